```python
import jax, jax.numpy as jnp
from jax import lax
import numpy as np

D_MODEL = 2048
BATCH = 4
SEQ = 4096
DEPTH = 4

MLA_HEADS = 16
MLA_Q_LORA = 512
MLA_KV_LORA = 512
MLA_NOPE_DIM = 128
MLA_ROPE_DIM = 64
MLA_V_DIM = 128
ROPE_THETA = 10000.0
MOBA_HEADS = 16
MOBA_HEAD_DIM = 128
MOBA_BLOCK = 256
MOBA_TOP_BLOCKS = 3
MOBA_QUERY_ROWS = 128
N_EXPERTS = 32
TOP_K = 4
EXPERT_FF = 1024
SWIGLU_LIMIT = 7.0
SWIGLU_ALPHA = 1.702
MOE_ROWS = 128
Q_BLOCK = 128
N_MLA_LAYERS = (DEPTH + 1) // 2
N_MOBA_LAYERS = DEPTH // 2
DEEPNORM_ALPHA = (2.0 * DEPTH) ** 0.25
DEEPNORM_BETA = (8.0 * DEPTH) ** -0.25
NEG_INF = -1e30
LN_EPS = 1e-5
RMS_EPS = 1e-6

kernel_name = "hybrid_mla_moba_moe_deepnorm"


def layer_norm(x, g, b):
    xf = x.astype(jnp.float32)
    mu = jnp.mean(xf, axis=-1, keepdims=True)
    var = jnp.mean(jnp.square(xf - mu), axis=-1, keepdims=True)
    return ((xf - mu) * lax.rsqrt(var + LN_EPS) * g + b).astype(x.dtype)


def rms_norm(x, g):
    xf = x.astype(jnp.float32)
    return (xf * lax.rsqrt(jnp.mean(jnp.square(xf), axis=-1, keepdims=True) + RMS_EPS) * g).astype(x.dtype)


def rope_tables(positions):
    inv_freq = ROPE_THETA ** (-jnp.arange(0, MLA_ROPE_DIM, 2, dtype=jnp.float32) / MLA_ROPE_DIM)
    ang = positions.astype(jnp.float32)[..., None] * inv_freq
    return jnp.cos(ang), jnp.sin(ang)


def apply_rope(x, cos, sin):
    x1, x2 = jnp.split(x.astype(jnp.float32), 2, axis=-1)
    return jnp.concatenate([x1 * cos - x2 * sin, x2 * cos + x1 * sin], axis=-1).astype(x.dtype)


def alibi_slopes(n_heads):
    return 2.0 ** (-8.0 * jnp.arange(1, n_heads + 1, dtype=jnp.float32) / n_heads)


def attend(s, v, eq):
    m = jnp.max(s, axis=-1, keepdims=True)
    p = jnp.exp(s - m)
    l = jnp.sum(p, axis=-1)
    o = jnp.einsum(eq, p.astype(v.dtype), v).astype(jnp.float32) / l[..., None]
    return o, m[..., 0] + jnp.log(l)


def group_rows(group_ids, num_groups, rows):
    n = group_ids.shape[0]
    n_blocks = -(-n // rows) + num_groups
    order = jnp.argsort(group_ids)
    sorted_g = group_ids[order]
    counts = jnp.zeros((num_groups,), jnp.int32).at[group_ids].add(1)
    padded = (counts + rows - 1) // rows * rows
    start = jnp.cumsum(counts) - counts
    pstart = jnp.cumsum(padded) - padded
    dest = pstart[sorted_g] + (jnp.arange(n, dtype=jnp.int32) - start[sorted_g])
    row_src = jnp.zeros((n_blocks * rows,), jnp.int32).at[dest].set(order.astype(jnp.int32))
    row_of = jnp.zeros((n,), jnp.int32).at[order].set(dest)
    block_starts = jnp.arange(n_blocks, dtype=jnp.int32) * rows
    block_group = jnp.searchsorted(jnp.cumsum(padded), block_starts, side='right')
    block_group = jnp.minimum(block_group, num_groups - 1).astype(jnp.int32)
    return row_src, block_group, row_of


def mla_attention(x, positions, w_in, g_q, g_kv, w_qb, w_kvb, w_o):
    B, S, _ = x.shape
    H = MLA_HEADS
    c = x @ w_in
    c_q, c_kv, k_rope = jnp.split(c, [MLA_Q_LORA, MLA_Q_LORA + MLA_KV_LORA], axis=-1)
    q = (rms_norm(c_q, g_q) @ w_qb).reshape(B, S, H, MLA_NOPE_DIM + MLA_ROPE_DIM)
    kv = (rms_norm(c_kv, g_kv) @ w_kvb).reshape(B, S, H, MLA_NOPE_DIM + MLA_V_DIM)
    q_nope, q_rope = jnp.split(q, [MLA_NOPE_DIM], axis=-1)
    k_nope, v = jnp.split(kv, [MLA_NOPE_DIM], axis=-1)
    cos, sin = rope_tables(positions)
    q_rope = apply_rope(q_rope, cos[:, :, None], sin[:, :, None])
    k_rope = apply_rope(k_rope, cos, sin)
    scale = (MLA_NOPE_DIM + MLA_ROPE_DIM) ** -0.5
    nqb = S // Q_BLOCK
    qn_b = q_nope.reshape(B, nqb, Q_BLOCK, H, MLA_NOPE_DIM).swapaxes(0, 1)
    qr_b = q_rope.reshape(B, nqb, Q_BLOCK, H, MLA_ROPE_DIM).swapaxes(0, 1)
    key_idx = jnp.arange(S)

    def q_block(args):
        qn, qr, i = args
        s = (jnp.einsum('bqhd,bkhd->bhqk', qn, k_nope)
             + jnp.einsum('bqhr,bkr->bhqk', qr, k_rope)).astype(jnp.float32) * scale
        q_idx = i * Q_BLOCK + jnp.arange(Q_BLOCK)
        s = jnp.where(key_idx[None, :] <= q_idx[:, None], s, NEG_INF)
        p = jax.nn.softmax(s, axis=-1)
        return jnp.einsum('bhqk,bkhd->bqhd', p.astype(v.dtype), v)

    o = lax.map(q_block, (qn_b, qr_b, jnp.arange(nqb)))
    o = o.swapaxes(0, 1).reshape(B, S, H * MLA_V_DIM)
    return o @ w_o


def moba_attention(x, positions, w_qkv, w_o):
    B, S, _ = x.shape
    H, Dh, BS = MOBA_HEADS, MOBA_HEAD_DIM, MOBA_BLOCK
    nb = -(-S // BS)
    S_pad = nb * BS
    k_sel = min(MOBA_TOP_BLOCKS, nb)
    scale = Dh ** -0.5
    qkv = (x @ w_qkv).reshape(B, S, 3, H, Dh)
    qkv = jnp.pad(qkv, ((0, 0), (0, S_pad - S), (0, 0), (0, 0), (0, 0)))
    q = jnp.moveaxis(qkv[:, :, 0], 1, 2)
    k = jnp.moveaxis(qkv[:, :, 1], 1, 2)
    v = jnp.moveaxis(qkv[:, :, 2], 1, 2)
    pos = jnp.pad(positions, ((0, 0), (0, S_pad - S)), mode='edge').astype(jnp.float32)
    slopes = alibi_slopes(H)
    qb = q.reshape(B, H, nb, BS, Dh)
    kb = k.reshape(B, H, nb, BS, Dh)
    vb = v.reshape(B, H, nb, BS, Dh)
    posb = pos.reshape(B, nb, BS)

    kmean = jnp.mean(kb.astype(jnp.float32), axis=3)
    gate = jnp.einsum('bhsd,bhmd->bhsm', q.astype(jnp.float32), kmean)
    q_blk = jnp.arange(S_pad) // BS
    gate = jnp.where(jnp.arange(nb)[None, :] < q_blk[:, None], gate, NEG_INF)
    _, sel = lax.top_k(gate, k_sel)
    sel_valid = sel < q_blk[:, None]

    s_own = jnp.einsum('bhnqd,bhnkd->bhnqk', qb, kb).astype(jnp.float32) * scale
    dist = posb[:, :, :, None] - posb[:, :, None, :]
    s_own = s_own - slopes[None, :, None, None, None] * dist[:, None]
    s_own = jnp.where(jnp.tril(jnp.ones((BS, BS), bool)), s_own, NEG_INF)
    o_own, lse_own = attend(s_own, vb, 'bhnqk,bhnkd->bhnqd')
    o_own = o_own.reshape(B, H, S_pad, Dh)
    lse_own = lse_own.reshape(B, H, S_pad, 1)

    G = B * H * nb
    g_ids = ((jnp.arange(B)[:, None, None, None] * H + jnp.arange(H)[None, :, None, None]) * nb
             + sel).reshape(-1).astype(jnp.int32)
    row_src, block_g, row_of = group_rows(g_ids, G, MOBA_QUERY_ROWS)
    n_blk = block_g.shape[0]
    q_src = row_src // k_sel
    q_rows = q.reshape(-1, Dh)[q_src].reshape(n_blk, MOBA_QUERY_ROWS, Dh)
    pq_rows = jnp.broadcast_to(pos[:, None, :], (B, H, S_pad)).reshape(-1)[q_src]
    pq_rows = pq_rows.reshape(n_blk, MOBA_QUERY_ROWS)
    k_flat = kb.reshape(G, BS, Dh)
    v_flat = vb.reshape(G, BS, Dh)
    pk_flat = jnp.broadcast_to(posb[:, None], (B, H, nb, BS)).reshape(G, BS)
    slope_flat = jnp.broadcast_to(slopes[None, :, None], (B, H, nb)).reshape(G)

    def sel_block(args):
        qr, pq, g = args
        s = (qr @ k_flat[g].T).astype(jnp.float32) * scale
        s = s - slope_flat[g] * (pq[:, None] - pk_flat[g][None, :])
        return attend(s, v_flat[g], 'qk,kd->qd')

    o_rows, lse_rows = lax.map(sel_block, (q_rows, pq_rows, block_g))
    o_sel = o_rows.reshape(-1, Dh)[row_of].reshape(B, H, S_pad, k_sel, Dh)
    lse_sel = lse_rows.reshape(-1)[row_of].reshape(B, H, S_pad, k_sel)
    lse_sel = jnp.where(sel_valid, lse_sel, NEG_INF)

    w = jax.nn.softmax(jnp.concatenate([lse_own, lse_sel], axis=-1), axis=-1)
    o = w[..., :1] * o_own + jnp.einsum('bhsk,bhskd->bhsd', w[..., 1:], o_sel)
    o = o[:, :, :S].astype(x.dtype)
    return jnp.moveaxis(o, 1, 2).reshape(B, S, H * Dh) @ w_o


def moe_ffn(x2, w_router, b_router, w_gu, b_gu, w_down, b_down):
    T, D = x2.shape
    logits = (x2 @ w_router + b_router).astype(jnp.float32)
    top_val, top_idx = lax.top_k(logits, TOP_K)
    gates = jax.nn.softmax(top_val, axis=-1).astype(x2.dtype)
    row_src, block_e, row_of = group_rows(top_idx.reshape(-1).astype(jnp.int32), N_EXPERTS, MOE_ROWS)
    n_blk = block_e.shape[0]
    xs = x2[row_src // TOP_K].reshape(n_blk, MOE_ROWS, D)

    def expert_block(args):
        xb, e = args
        h = xb @ w_gu[e] + b_gu[e]
        x_glu, x_lin = jnp.split(h, 2, axis=-1)
        x_glu = jnp.minimum(x_glu, SWIGLU_LIMIT)
        x_lin = jnp.clip(x_lin, -SWIGLU_LIMIT, SWIGLU_LIMIT)
        act = x_glu * jax.nn.sigmoid(SWIGLU_ALPHA * x_glu) * (x_lin + 1.0)
        return act @ w_down[e] + b_down[e]

    ys = lax.map(expert_block, (xs, block_e)).reshape(n_blk * MOE_ROWS, D)
    y_assign = ys[row_of].reshape(T, TOP_K, D)
    return jnp.einsum('tk,tkd->td', gates, y_assign)


def setup_inputs(seed: int = 0) -> dict:
    key = jax.random.key(seed)
    ks = jax.random.split(key, 20)
    f32 = jnp.float32
    nrm = lambda k, shape, s: jax.random.normal(k, shape, f32) * s
    LA, LB, H = N_MLA_LAYERS, N_MOBA_LAYERS, MLA_HEADS
    beta = DEEPNORM_BETA
    x = jax.random.normal(ks[0], (BATCH, SEQ, D_MODEL), f32)
    positions = (jax.random.randint(ks[1], (BATCH, 1), 0, 1024, dtype=jnp.int32)
                 + jnp.arange(SEQ, dtype=jnp.int32)[None, :])
    mla_w_in = nrm(ks[2], (LA, D_MODEL, MLA_Q_LORA + MLA_KV_LORA + MLA_ROPE_DIM), D_MODEL ** -0.5)
    mla_g_q = 1.0 + nrm(ks[3], (LA, MLA_Q_LORA), 0.02)
    mla_g_kv = 1.0 + nrm(ks[4], (LA, MLA_KV_LORA), 0.02)
    mla_w_qb = nrm(ks[5], (LA, MLA_Q_LORA, H * (MLA_NOPE_DIM + MLA_ROPE_DIM)), MLA_Q_LORA ** -0.5)
    kv_col_scale = jnp.concatenate([jnp.ones((MLA_NOPE_DIM,), f32), jnp.full((MLA_V_DIM,), beta, f32)])
    mla_w_kvb = (nrm(ks[6], (LA, MLA_KV_LORA, H, MLA_NOPE_DIM + MLA_V_DIM), MLA_KV_LORA ** -0.5)
                 * kv_col_scale).reshape(LA, MLA_KV_LORA, H * (MLA_NOPE_DIM + MLA_V_DIM))
    mla_w_o = nrm(ks[7], (LA, H * MLA_V_DIM, D_MODEL), (H * MLA_V_DIM) ** -0.5 * beta)
    qkv_scale = jnp.array([1.0, 1.0, beta], f32)[:, None]
    moba_w_qkv = (nrm(ks[8], (LB, D_MODEL, 3, MOBA_HEADS * MOBA_HEAD_DIM), D_MODEL ** -0.5)
                  * qkv_scale).reshape(LB, D_MODEL, 3 * MOBA_HEADS * MOBA_HEAD_DIM)
    moba_w_o = nrm(ks[9], (LB, MOBA_HEADS * MOBA_HEAD_DIM, D_MODEL), (MOBA_HEADS * MOBA_HEAD_DIM) ** -0.5 * beta)
    ln1_g = 1.0 + nrm(ks[10], (DEPTH, D_MODEL), 0.02)
    ln1_b = nrm(ks[11], (DEPTH, D_MODEL), 0.02)
    ln2_g = 1.0 + nrm(ks[12], (DEPTH, D_MODEL), 0.02)
    ln2_b = nrm(ks[13], (DEPTH, D_MODEL), 0.02)
    moe_w_router = nrm(ks[14], (DEPTH, D_MODEL, N_EXPERTS), D_MODEL ** -0.5)
    moe_b_router = nrm(ks[15], (DEPTH, N_EXPERTS), 0.01)
    moe_w_gu = nrm(ks[16], (DEPTH, N_EXPERTS, D_MODEL, 2 * EXPERT_FF), D_MODEL ** -0.5)
    moe_b_gu = nrm(ks[17], (DEPTH, N_EXPERTS, 2 * EXPERT_FF), 0.02)
    moe_w_down = nrm(ks[18], (DEPTH, N_EXPERTS, EXPERT_FF, D_MODEL), EXPERT_FF ** -0.5 * beta)
    moe_b_down = nrm(ks[19], (DEPTH, N_EXPERTS, D_MODEL), 0.02)
    return {"x": x, "positions": positions,
            "mla_w_in": mla_w_in, "mla_g_q": mla_g_q, "mla_g_kv": mla_g_kv,
            "mla_w_qb": mla_w_qb, "mla_w_kvb": mla_w_kvb, "mla_w_o": mla_w_o,
            "moba_w_qkv": moba_w_qkv, "moba_w_o": moba_w_o,
            "ln1_g": ln1_g, "ln1_b": ln1_b, "ln2_g": ln2_g, "ln2_b": ln2_b,
            "moe_w_router": moe_w_router, "moe_b_router": moe_b_router,
            "moe_w_gu": moe_w_gu, "moe_b_gu": moe_b_gu,
            "moe_w_down": moe_w_down, "moe_b_down": moe_b_down}


def reference(x, positions, mla_w_in, mla_g_q, mla_g_kv, mla_w_qb, mla_w_kvb, mla_w_o,
              moba_w_qkv, moba_w_o, ln1_g, ln1_b, ln2_g, ln2_b,
              moe_w_router, moe_b_router, moe_w_gu, moe_b_gu, moe_w_down, moe_b_down):
    B, S, D = x.shape
    for i in range(DEPTH):
        j = i // 2
        if i % 2 == 0:
            y = mla_attention(x, positions, mla_w_in[j], mla_g_q[j], mla_g_kv[j],
                              mla_w_qb[j], mla_w_kvb[j], mla_w_o[j])
        else:
            y = moba_attention(x, positions, moba_w_qkv[j], moba_w_o[j])
        x = layer_norm(DEEPNORM_ALPHA * x + y, ln1_g[i], ln1_b[i])
        y = moe_ffn(x.reshape(B * S, D), moe_w_router[i], moe_b_router[i], moe_w_gu[i],
                    moe_b_gu[i], moe_w_down[i], moe_b_down[i]).reshape(B, S, D)
        x = layer_norm(DEEPNORM_ALPHA * x + y, ln2_g[i], ln2_b[i])
    return x
```

```python
import functools

import jax
import jax.numpy as jnp
from jax import lax
from jax.experimental import pallas as pl
from jax.experimental.pallas import tpu as pltpu

F32 = jnp.float32
BF16 = jnp.bfloat16
I32 = jnp.int32
U32 = jnp.uint32

HEADS = 16
Q_LORA = 512
KV_LORA = 512
NOPE = 128
ROPE = 64
V_DIM = 128
ROPE_THETA = 10000.0
MOBA_DH = 128
MOBA_BLOCK = 256
MOBA_TOP = 3
N_EXPERTS = 32
TOP_K = 4
SWIGLU_LIMIT = 7.0
SWIGLU_ALPHA = 1.702
NEG_INF = -1e30
LN_EPS = 1e-5
RMS_EPS = 1e-6

LANES = 128
HEAD_PAD = 256
VMEM_LIMIT = 52 * 1024 * 1024

TM_PROJ = 256
TQ_MLA = 512
MOE_ROWS = 256
FF_CHUNK = 512
TM_COMBINE = 128


def _cparams(sem):
    return pltpu.CompilerParams(dimension_semantics=sem, vmem_limit_bytes=VMEM_LIMIT)


def _dot(a, b):
    return jnp.dot(a, b, preferred_element_type=F32)


def _dot_nt(a, b):
    return lax.dot_general(a, b, (((1,), (1,)), ((), ())), preferred_element_type=F32)


def _rope_rotate(r, c_tab, s_tab):
    swapped = pltpu.roll(r, 32, axis=1) + pltpu.roll(r, 96, axis=1)
    return r * c_tab + swapped * s_tab


def _layer_norm(z, g, b):
    mu = jnp.mean(z, axis=-1, keepdims=True)
    zc = z - mu
    var = jnp.mean(zc * zc, axis=-1, keepdims=True)
    return zc * lax.rsqrt(var + LN_EPS) * g + b


def _mla_in_kernel(x_ref, w_ref, gq_ref, gkv_ref, c_ref, s_ref, cq_ref, ckv_ref, kr_ref):
    acc = _dot(x_ref[...], w_ref[...])
    cq = acc[:, :Q_LORA]
    ckv = acc[:, Q_LORA:Q_LORA + KV_LORA]
    kr = acc[:, Q_LORA + KV_LORA:]
    cq_n = cq * lax.rsqrt(jnp.mean(cq * cq, axis=-1, keepdims=True) + RMS_EPS) * gq_ref[...]
    ckv_n = ckv * lax.rsqrt(jnp.mean(ckv * ckv, axis=-1, keepdims=True) + RMS_EPS) * gkv_ref[...]
    cq_ref[...] = cq_n.astype(BF16)
    ckv_ref[...] = ckv_n.astype(BF16)
    kr_ref[...] = _rope_rotate(kr, c_ref[...], s_ref[...]).astype(BF16)


def _mla_in_proj(xb, w_in_p, g_q, g_kv, c_tab, s_tab):
    T, D = xb.shape
    n_out = w_in_p.shape[1]
    tm = TM_PROJ
    return pl.pallas_call(
        _mla_in_kernel,
        grid=(T // tm,),
        in_specs=[
            pl.BlockSpec((tm, D), lambda i: (i, 0)),
            pl.BlockSpec((D, n_out), lambda i: (0, 0)),
            pl.BlockSpec((1, Q_LORA), lambda i: (0, 0)),
            pl.BlockSpec((1, KV_LORA), lambda i: (0, 0)),
            pl.BlockSpec((tm, LANES), lambda i: (i, 0)),
            pl.BlockSpec((tm, LANES), lambda i: (i, 0)),
        ],
        out_specs=[
            pl.BlockSpec((tm, Q_LORA), lambda i: (i, 0)),
            pl.BlockSpec((tm, KV_LORA), lambda i: (i, 0)),
            pl.BlockSpec((tm, LANES), lambda i: (i, 0)),
        ],
        out_shape=[
            jax.ShapeDtypeStruct((T, Q_LORA), BF16),
            jax.ShapeDtypeStruct((T, KV_LORA), BF16),
            jax.ShapeDtypeStruct((T, LANES), BF16),
        ],
        compiler_params=_cparams(("arbitrary",)),
        name="mla_in_proj",
    )(xb, w_in_p, g_q, g_kv, c_tab, s_tab)


def _mla_q_kernel(cq_ref, w_ref, c_ref, s_ref, q_ref, *, scale):
    acc = _dot(cq_ref[...], w_ref[...])
    c_tab = c_ref[...]
    s_tab = s_ref[...]
    for h in range(HEADS):
        base = h * HEAD_PAD
        q_ref[0, h, :, :NOPE] = (acc[:, base:base + NOPE] * scale).astype(BF16)
        rot = _rope_rotate(acc[:, base + NOPE:base + HEAD_PAD], c_tab, s_tab)
        q_ref[0, h, :, NOPE:] = (rot * scale).astype(BF16)


def _mla_q_proj(cq_n, w_q_p, c_tab, s_tab, B, S):
    T = cq_n.shape[0]
    tm = TM_PROJ
    spb = S // tm
    scale = float((NOPE + ROPE) ** -0.5)
    return pl.pallas_call(
        functools.partial(_mla_q_kernel, scale=scale),
        grid=(T // tm,),
        in_specs=[
            pl.BlockSpec((tm, Q_LORA), lambda i: (i, 0)),
            pl.BlockSpec((Q_LORA, HEADS * HEAD_PAD), lambda i: (0, 0)),
            pl.BlockSpec((tm, LANES), lambda i: (i, 0)),
            pl.BlockSpec((tm, LANES), lambda i: (i, 0)),
        ],
        out_specs=pl.BlockSpec((1, HEADS, tm, HEAD_PAD), lambda i: (i // spb, 0, i % spb, 0)),
        out_shape=jax.ShapeDtypeStruct((B, HEADS, S, HEAD_PAD), BF16),
        compiler_params=_cparams(("arbitrary",)),
        name="mla_q_proj",
    )(cq_n, w_q_p, c_tab, s_tab)


def _mla_kv_kernel(ckv_ref, w_ref, kr_ref, k_ref, v_ref):
    acc = _dot(ckv_ref[...], w_ref[...])
    kr = kr_ref[...]
    for h in range(HEADS):
        base = h * (NOPE + V_DIM)
        k_ref[0, h, :, :NOPE] = acc[:, base:base + NOPE].astype(BF16)
        k_ref[0, h, :, NOPE:] = kr
        v_ref[0, h] = acc[:, base + NOPE:base + NOPE + V_DIM].astype(BF16)


def _mla_kv_proj(ckv_n, w_kvb, kr, B, S):
    T = ckv_n.shape[0]
    tm = TM_PROJ
    spb = S // tm
    n_out = HEADS * (NOPE + V_DIM)
    return pl.pallas_call(
        _mla_kv_kernel,
        grid=(T // tm,),
        in_specs=[
            pl.BlockSpec((tm, KV_LORA), lambda i: (i, 0)),
            pl.BlockSpec((KV_LORA, n_out), lambda i: (0, 0)),
            pl.BlockSpec((tm, LANES), lambda i: (i, 0)),
        ],
        out_specs=[
            pl.BlockSpec((1, HEADS, tm, HEAD_PAD), lambda i: (i // spb, 0, i % spb, 0)),
            pl.BlockSpec((1, HEADS, tm, V_DIM), lambda i: (i // spb, 0, i % spb, 0)),
        ],
        out_shape=[
            jax.ShapeDtypeStruct((B, HEADS, S, HEAD_PAD), BF16),
            jax.ShapeDtypeStruct((B, HEADS, S, V_DIM), BF16),
        ],
        compiler_params=_cparams(("arbitrary",)),
        name="mla_kv_proj",
    )(ckv_n, w_kvb, kr)


def _mla_attn_kernel(q_ref, k_ref, v_ref, o_ref, *, tq):
    qi = pl.program_id(2)
    q = q_ref[0, 0]

    def step(kj, carry, diagonal):
        m, l, acc = carry
        start = pl.multiple_of(kj * tq, tq)
        k = k_ref[0, 0, pl.ds(start, tq), :]
        v = v_ref[0, 0, pl.ds(start, tq), :]
        s = _dot_nt(q, k)
        if diagonal:
            row = lax.broadcasted_iota(I32, s.shape, 0)
            col = lax.broadcasted_iota(I32, s.shape, 1)
            s = jnp.where(col <= row, s, NEG_INF)
        m_new = jnp.maximum(m, jnp.max(s, axis=-1, keepdims=True))
        alpha = jnp.exp(m - m_new)
        p = jnp.exp(s - m_new)
        l = alpha * l + jnp.sum(p, axis=-1, keepdims=True)
        acc = alpha * acc + _dot(p.astype(BF16), v)
        return m_new, l, acc

    init = (jnp.full((tq, 1), NEG_INF, F32), jnp.zeros((tq, 1), F32), jnp.zeros((tq, V_DIM), F32))
    carry = lax.fori_loop(0, qi, lambda kj, c: step(kj, c, False), init)
    _, l, acc = step(qi, carry, True)
    o_ref[0] = (acc / l).astype(BF16)


def _mla_attention(q, k, v):
    B, H, S, _ = q.shape
    tq = min(TQ_MLA, S)
    return pl.pallas_call(
        functools.partial(_mla_attn_kernel, tq=tq),
        grid=(B, H, S // tq),
        in_specs=[
            pl.BlockSpec((1, 1, tq, HEAD_PAD), lambda b, h, i: (b, h, i, 0)),
            pl.BlockSpec((1, 1, S, HEAD_PAD), lambda b, h, i: (b, h, 0, 0)),
            pl.BlockSpec((1, 1, S, V_DIM), lambda b, h, i: (b, h, 0, 0)),
        ],
        out_specs=pl.BlockSpec((1, tq, V_DIM), lambda b, h, i: (b, i, h)),
        out_shape=jax.ShapeDtypeStruct((B, S, H * V_DIM), BF16),
        compiler_params=_cparams(("arbitrary", "arbitrary", "arbitrary")),
        name="mla_attention",
    )(q, k, v)


def _moba_qkv_kernel(x_ref, w_ref, qkv_ref, cm_ref, *, tm, heads_per_tile, scale):
    j = pl.program_id(0)
    acc = _dot(x_ref[...], w_ref[...])
    sc = jnp.where(j < (HEADS // heads_per_tile), scale, 1.0).astype(F32)
    for h in range(heads_per_tile):
        qkv_ref[0, 0, h] = (acc[:, h * MOBA_DH:(h + 1) * MOBA_DH] * sc).astype(BF16)
    for r in range(tm // MOBA_BLOCK):
        cm_ref[0, r] = jnp.mean(acc[r * MOBA_BLOCK:(r + 1) * MOBA_BLOCK], axis=0, keepdims=True)


def _moba_qkv_proj(xb, w_qkv_b, B, S):
    T, D = xb.shape
    tm = 2 * MOBA_BLOCK
    hpt = 8
    tn = hpt * MOBA_DH
    tiles_per_sec = HEADS // hpt
    spb = S // tm
    nbt = tm // MOBA_BLOCK
    scale = float(MOBA_DH ** -0.5)
    return pl.pallas_call(
        functools.partial(_moba_qkv_kernel, tm=tm, heads_per_tile=hpt, scale=scale),
        grid=(3 * tiles_per_sec, T // tm),
        in_specs=[
            pl.BlockSpec((tm, D), lambda j, i: (i, 0)),
            pl.BlockSpec((D, tn), lambda j, i: (0, j)),
        ],
        out_specs=[
            pl.BlockSpec((1, 1, hpt, tm, MOBA_DH),
                         lambda j, i: (j // tiles_per_sec, i // spb, j % tiles_per_sec, i % spb, 0)),
            pl.BlockSpec((1, nbt, 1, tn), lambda j, i: (j // tiles_per_sec, i, 0, j % tiles_per_sec)),
        ],
        out_shape=[
            jax.ShapeDtypeStruct((3, B, HEADS, S, MOBA_DH), BF16),
            jax.ShapeDtypeStruct((3, T // MOBA_BLOCK, 1, HEADS * MOBA_DH), F32),
        ],
        compiler_params=_cparams(("arbitrary", "arbitrary")),
        name="moba_qkv_proj",
    )(xb, w_qkv_b)


def _moba_attn_kernel(slope_ref, q_ref, k_ref, v_ref, km_ref, pq_ref, pk_ref, o_ref, *, nb):
    h = pl.program_id(1)
    qi = pl.program_id(2)
    bs = MOBA_BLOCK
    slope = slope_ref[h]
    q = q_ref[0, 0, 0]
    pq = pq_ref[0]

    km = km_ref[0, :, 0, :]
    km_hi = km.astype(BF16)
    rem = km - km_hi.astype(F32)
    km_mid = rem.astype(BF16)
    km_lo = (rem - km_mid.astype(F32)).astype(BF16)
    gate = _dot_nt(q, km_hi) + _dot_nt(q, km_mid) + _dot_nt(q, km_lo)
    blk = lax.broadcasted_iota(I32, gate.shape, 1)
    past = blk < qi
    work = jnp.where(past, gate, NEG_INF)
    sel = jnp.zeros(gate.shape, F32)
    for _ in range(min(MOBA_TOP, nb)):
        mx = jnp.max(work, axis=-1, keepdims=True)
        ix = jnp.min(jnp.where(work == mx, blk, nb), axis=-1, keepdims=True)
        hit = blk == ix
        sel = jnp.where(hit & past, 1.0, sel)
        work = jnp.where(hit, -jnp.inf, work)

    def scores(kj):
        start = pl.multiple_of(kj * bs, bs)
        k = k_ref[0, 0, 0, pl.ds(start, bs), :]
        v = v_ref[0, 0, 0, pl.ds(start, bs), :]
        pk = pk_ref[0, pl.ds(kj, 1), :]
        s = _dot_nt(q, k) - slope * (pq - pk)
        return s, v

    s, v = scores(qi)
    row = lax.broadcasted_iota(I32, s.shape, 0)
    col = lax.broadcasted_iota(I32, s.shape, 1)
    s = jnp.where(col <= row, s, NEG_INF)
    m0 = jnp.max(s, axis=-1, keepdims=True)
    p = jnp.exp(s - m0)
    l0 = jnp.sum(p, axis=-1, keepdims=True)
    acc0 = _dot(p.astype(BF16), v)

    def past_block(kj, carry):
        m, l, acc = carry
        s, v = scores(kj)
        chosen = jnp.max(jnp.where(blk == kj, sel, 0.0), axis=-1, keepdims=True)
        s = jnp.where(chosen > 0.0, s, NEG_INF)
        m_new = jnp.maximum(m, jnp.max(s, axis=-1, keepdims=True))
        alpha = jnp.exp(m - m_new)
        p = jnp.exp(s - m_new)
        l = alpha * l + jnp.sum(p, axis=-1, keepdims=True)
        acc = alpha * acc + _dot(p.astype(BF16), v)
        return m_new, l, acc

    _, l, acc = lax.fori_loop(0, qi, past_block, (m0, l0, acc0))
    o_ref[0] = (acc / l).astype(BF16)


def _moba_attention(qkv, colmean, pos_col, pos_blk, slopes):
    _, B, H, S, Dh = qkv.shape
    bs = MOBA_BLOCK
    nb = S // bs
    grid_spec = pltpu.PrefetchScalarGridSpec(
        num_scalar_prefetch=1,
        grid=(B, H, nb),
        in_specs=[
            pl.BlockSpec((1, 1, 1, bs, Dh), lambda b, h, i, sl: (0, b, h, i, 0)),
            pl.BlockSpec((1, 1, 1, S, Dh), lambda b, h, i, sl: (1, b, h, 0, 0)),
            pl.BlockSpec((1, 1, 1, S, Dh), lambda b, h, i, sl: (2, b, h, 0, 0)),
            pl.BlockSpec((1, nb, 1, Dh), lambda b, h, i, sl: (1, b, 0, h)),
            pl.BlockSpec((1, bs, 1), lambda b, h, i, sl: (b, i, 0)),
            pl.BlockSpec((1, nb, bs), lambda b, h, i, sl: (b, 0, 0)),
        ],
        out_specs=pl.BlockSpec((1, bs, Dh), lambda b, h, i, sl: (b, i, h)),
    )
    return pl.pallas_call(
        functools.partial(_moba_attn_kernel, nb=nb),
        grid_spec=grid_spec,
        out_shape=jax.ShapeDtypeStruct((B, S, H * Dh), BF16),
        compiler_params=_cparams(("arbitrary", "arbitrary", "arbitrary")),
        name="moba_attention",
    )(slopes, qkv, qkv, qkv, colmean, pos_col, pos_blk)


def _route(xn, wrh_ref, wrl_ref, br_ref, idx_ref, gate_ref):
    x_hi = xn.astype(BF16)
    x_lo = (xn - x_hi.astype(F32)).astype(BF16)
    wrh = wrh_ref[...]
    logits = _dot(x_hi, wrh) + _dot(x_lo, wrh) + _dot(x_hi, wrl_ref[...]) + br_ref[...]
    lane = lax.broadcasted_iota(I32, logits.shape, 1)
    work = jnp.where(lane < N_EXPERTS, logits, -jnp.inf)
    vals, idxs = [], []
    for _ in range(TOP_K):
        mx = jnp.max(work, axis=-1, keepdims=True)
        ix = jnp.min(jnp.where(work == mx, lane, LANES), axis=-1, keepdims=True)
        vals.append(mx)
        idxs.append(ix)
        work = jnp.where(lane == ix, -jnp.inf, work)
    exps = [jnp.exp(v - vals[0]) for v in vals]
    den = exps[0] + exps[1] + exps[2] + exps[3]
    idx_out = jnp.zeros(logits.shape, I32)
    gate_out = jnp.zeros(logits.shape, F32)
    for k in range(TOP_K):
        idx_out = jnp.where(lane == k, idxs[k], idx_out)
        gate_out = jnp.where(lane == k, exps[k] / den, gate_out)
    idx_ref[...] = idx_out
    gate_ref[...] = gate_out


def _pack_bf16_pairs(xn):
    half = xn.shape[1] // 2
    bits = lax.bitcast_convert_type(xn.astype(BF16).astype(F32), U32)
    return (bits[:, :half] >> 16) | (bits[:, half:] & jnp.uint32(0xFFFF0000))


def _attn_out_kernel(o_ref, wo_ref, x_ref, g_ref, b_ref, wrh_ref, wrl_ref, br_ref,
                     xo_ref, xp_ref, idx_ref, gate_ref, *, alpha):
    y = _dot(o_ref[...], wo_ref[...])
    xn = _layer_norm(alpha * x_ref[...] + y, g_ref[...], b_ref[...])
    xo_ref[...] = xn
    xp_ref[...] = _pack_bf16_pairs(xn)
    _route(xn, wrh_ref, wrl_ref, br_ref, idx_ref, gate_ref)


def _attn_out(o, w_o_b, x, g, b, wr_hi, wr_lo, br, alpha):
    T, D = x.shape
    tm = TM_PROJ
    row = lambda i: (i, 0)
    const = lambda i: (0, 0)
    return pl.pallas_call(
        functools.partial(_attn_out_kernel, alpha=alpha),
        grid=(T // tm,),
        in_specs=[
            pl.BlockSpec((tm, D), row),
            pl.BlockSpec((D, D), const),
            pl.BlockSpec((tm, D), row),
            pl.BlockSpec((1, D), const),
            pl.BlockSpec((1, D), const),
            pl.BlockSpec((D, LANES), const),
            pl.BlockSpec((D, LANES), const),
            pl.BlockSpec((1, LANES), const),
        ],
        out_specs=[
            pl.BlockSpec((tm, D), row),
            pl.BlockSpec((tm, D // 2), row),
            pl.BlockSpec((tm, LANES), row),
            pl.BlockSpec((tm, LANES), row),
        ],
        out_shape=[
            jax.ShapeDtypeStruct((T, D), F32),
            jax.ShapeDtypeStruct((T, D // 2), U32),
            jax.ShapeDtypeStruct((T, LANES), I32),
            jax.ShapeDtypeStruct((T, LANES), F32),
        ],
        compiler_params=_cparams(("arbitrary",)),
        name="attn_out_ln_router",
    )(o, w_o_b, x, g, b, wr_hi, wr_lo, br)


def _moe_gather_kernel(tok_ref, x_hbm, o_hbm, sem, *, rows):
    i = pl.program_id(0)
    base = i * rows

    def issue(r, c):
        t = tok_ref[0, 0, r]
        pltpu.make_async_copy(x_hbm.at[pl.ds(t, 1), :], o_hbm.at[pl.ds(base + r, 1), :], sem).start()
        return c

    lax.fori_loop(0, rows, issue, 0)
    pltpu.make_async_copy(x_hbm.at[pl.ds(0, rows), :], o_hbm.at[pl.ds(base, rows), :], sem).wait()


def _moe_gather(xp, row_tok, rows):
    n_blocks = row_tok.shape[0] // rows
    width = xp.shape[1]
    return pl.pallas_call(
        functools.partial(_moe_gather_kernel, rows=rows),
        grid=(n_blocks,),
        in_specs=[
            pl.BlockSpec((1, 1, rows), lambda i: (i, 0, 0), memory_space=pltpu.SMEM),
            pl.BlockSpec(memory_space=pl.ANY),
        ],
        out_specs=pl.BlockSpec(memory_space=pl.ANY),
        out_shape=jax.ShapeDtypeStruct((n_blocks * rows, width), U32),
        scratch_shapes=[pltpu.SemaphoreType.DMA(())],
        compiler_params=_cparams(("arbitrary",)),
        name="moe_gather",
    )(row_tok.reshape(n_blocks, 1, rows), xp)


def _moe_up_kernel(be_ref, xs_ref, wg_ref, wl_ref, bg_ref, bl_ref, h_ref, wg_s, wl_s):
    i = pl.program_id(1)
    e = be_ref[i]
    prev = be_ref[jnp.maximum(i - 1, 0)]

    @pl.when((i == 0) | (e != prev))
    def _():
        wg_s[...] = wg_ref[0].astype(BF16)
        wl_s[...] = wl_ref[0].astype(BF16)

    u = xs_ref[...]
    half = u.shape[1]
    x_lo = lax.bitcast_convert_type(u << 16, F32).astype(BF16)
    x_hi = lax.bitcast_convert_type(u & jnp.uint32(0xFFFF0000), F32).astype(BF16)
    g = _dot(x_lo, wg_s[:half]) + _dot(x_hi, wg_s[half:]) + bg_ref[0]
    lin = _dot(x_lo, wl_s[:half]) + _dot(x_hi, wl_s[half:]) + bl_ref[0]
    g = jnp.minimum(g, SWIGLU_LIMIT)
    lin = jnp.clip(lin, -SWIGLU_LIMIT, SWIGLU_LIMIT)
    act = g * jax.nn.sigmoid(SWIGLU_ALPHA * g) * (lin + 1.0)
    h_ref[...] = act.astype(BF16)


def _moe_up(xs, block_e, w_gu, b_gu3, rows):
    R, half = xs.shape
    E, D, two_ff = w_gu.shape
    ff = two_ff // 2
    fc = FF_CHUNK
    nj = ff // fc
    grid_spec = pltpu.PrefetchScalarGridSpec(
        num_scalar_prefetch=1,
        grid=(nj, R // rows),
        in_specs=[
            pl.BlockSpec((rows, half), lambda j, i, be: (i, 0)),
            pl.BlockSpec((1, D, fc), lambda j, i, be: (be[i], 0, j)),
            pl.BlockSpec((1, D, fc), lambda j, i, be: (be[i], 0, nj + j)),
            pl.BlockSpec((1, 1, fc), lambda j, i, be: (be[i], 0, j)),
            pl.BlockSpec((1, 1, fc), lambda j, i, be: (be[i], 0, nj + j)),
        ],
        out_specs=pl.BlockSpec((rows, fc), lambda j, i, be: (i, j)),
        scratch_shapes=[pltpu.VMEM((D, fc), BF16), pltpu.VMEM((D, fc), BF16)],
    )
    return pl.pallas_call(
        _moe_up_kernel,
        grid_spec=grid_spec,
        out_shape=jax.ShapeDtypeStruct((R, ff), BF16),
        compiler_params=_cparams(("arbitrary", "arbitrary")),
        name="moe_up",
    )(block_e, xs, w_gu, w_gu, b_gu3, b_gu3)


def _moe_down_kernel(be_ref, h_ref, wd_ref, bd_ref, y_ref, wd_s):
    i = pl.program_id(0)
    e = be_ref[i]
    prev = be_ref[jnp.maximum(i - 1, 0)]

    @pl.when((i == 0) | (e != prev))
    def _():
        wd_s[...] = wd_ref[0].astype(BF16)

    y_ref[...] = _dot(h_ref[...], wd_s[...]) + bd_ref[0]


def _moe_down(h, block_e, w_down, b_down3, rows):
    R, ff = h.shape
    E, _, D = w_down.shape
    grid_spec = pltpu.PrefetchScalarGridSpec(
        num_scalar_prefetch=1,
        grid=(R // rows,),
        in_specs=[
            pl.BlockSpec((rows, ff), lambda i, be: (i, 0)),
            pl.BlockSpec((1, ff, D), lambda i, be: (be[i], 0, 0)),
            pl.BlockSpec((1, 1, D), lambda i, be: (be[i], 0, 0)),
        ],
        out_specs=pl.BlockSpec((rows, D), lambda i, be: (i, 0)),
        scratch_shapes=[pltpu.VMEM((ff, D), BF16)],
    )
    return pl.pallas_call(
        _moe_down_kernel,
        grid_spec=grid_spec,
        out_shape=jax.ShapeDtypeStruct((R, D), F32),
        compiler_params=_cparams(("arbitrary",)),
        name="moe_down",
    )(block_e, h, w_down, b_down3)


def _moe_combine_kernel(cur_ref, nxt_ref, ys_hbm, gate_ref, x_ref, g_ref, b_ref, *rest,
                        tm, alpha, emit_bf16):
    if emit_bf16:
        xo_ref, xb_ref, buf, sem = rest
    else:
        xo_ref, buf, sem = rest
        xb_ref = None
    i = pl.program_id(0)
    n = pl.num_programs(0)
    slot = i % 2

    def issue(idx_ref, dst_slot):
        def body(r, c):
            for k in range(TOP_K):
                row = idx_ref[0, 0, r * TOP_K + k]
                pltpu.make_async_copy(ys_hbm.at[pl.ds(row, 1), :],
                                      buf.at[dst_slot, k, pl.ds(r, 1), :],
                                      sem.at[dst_slot]).start()
            return c
        lax.fori_loop(0, tm, body, 0)

    @pl.when(i == 0)
    def _():
        issue(cur_ref, 0)

    @pl.when(i + 1 < n)
    def _():
        issue(nxt_ref, 1 - slot)

    for k in range(TOP_K):
        pltpu.make_async_copy(ys_hbm.at[pl.ds(0, tm), :], buf.at[slot, k], sem.at[slot]).wait()

    gates = gate_ref[...]
    y = gates[:, 0:1] * buf[slot, 0]
    for k in range(1, TOP_K):
        y = y + gates[:, k:k + 1] * buf[slot, k]
    xn = _layer_norm(alpha * x_ref[...] + y, g_ref[...], b_ref[...])
    xo_ref[...] = xn
    if emit_bf16:
        xb_ref[...] = xn.astype(BF16)


def _moe_combine(ys, row_of, gates, x, g, b, alpha, emit_bf16):
    T, D = x.shape
    tm = TM_COMBINE
    n = T // tm
    idx3 = row_of.reshape(n, 1, tm * TOP_K)
    row = lambda i: (i, 0)
    const = lambda i: (0, 0)
    out_specs = [pl.BlockSpec((tm, D), row)]
    out_shape = [jax.ShapeDtypeStruct((T, D), F32)]
    if emit_bf16:
        out_specs.append(pl.BlockSpec((tm, D), row))
        out_shape.append(jax.ShapeDtypeStruct((T, D), BF16))
    outs = pl.pallas_call(
        functools.partial(_moe_combine_kernel, tm=tm, alpha=alpha, emit_bf16=emit_bf16),
        grid=(n,),
        in_specs=[
            pl.BlockSpec((1, 1, tm * TOP_K), lambda i: (i, 0, 0), memory_space=pltpu.SMEM),
            pl.BlockSpec((1, 1, tm * TOP_K), lambda i: (jnp.minimum(i + 1, n - 1), 0, 0),
                         memory_space=pltpu.SMEM),
            pl.BlockSpec(memory_space=pl.ANY),
            pl.BlockSpec((tm, LANES), row),
            pl.BlockSpec((tm, D), row),
            pl.BlockSpec((1, D), const),
            pl.BlockSpec((1, D), const),
        ],
        out_specs=out_specs,
        out_shape=out_shape,
        scratch_shapes=[pltpu.VMEM((2, TOP_K, tm, D), F32), pltpu.SemaphoreType.DMA((2,))],
        compiler_params=_cparams(("arbitrary",)),
        name="moe_combine_ln",
    )(idx3, idx3, ys, gates, x, g, b)
    return outs if emit_bf16 else (outs[0], None)


def _group_rows(top_idx, rows):
    T = top_idx.shape[0]
    A = T * TOP_K
    flat = top_idx.reshape(A)
    onehot = (flat[:, None] == jnp.arange(N_EXPERTS, dtype=I32)[None, :]).astype(I32)
    csum = jnp.cumsum(onehot, axis=0)
    counts = csum[-1]
    rank = jnp.take_along_axis(csum, flat[:, None], axis=1)[:, 0] - 1
    padded = (counts + rows - 1) // rows * rows
    pend = jnp.cumsum(padded)
    pstart = pend - padded
    row_of = (pstart[flat] + rank).astype(I32)
    n_blocks = A // rows + N_EXPERTS
    row_tok = jnp.zeros((n_blocks * rows,), I32).at[row_of].set(jnp.arange(A, dtype=I32) // TOP_K)
    starts = jnp.arange(n_blocks, dtype=I32) * rows
    block_e = jnp.minimum(jnp.searchsorted(pend, starts, side="right"), N_EXPERTS - 1).astype(I32)
    return row_tok, block_e, row_of


def _moe_ffn(xp, top_idx, gates, x, w_gu, b_gu, w_down, b_down, g, b, alpha, emit_bf16):
    rows = MOE_ROWS
    row_tok, block_e, row_of = _group_rows(top_idx, rows)
    xs = _moe_gather(xp, row_tok, rows)
    E = w_gu.shape[0]
    h = _moe_up(xs, block_e, w_gu, b_gu.reshape(E, 1, -1), rows)
    ys = _moe_down(h, block_e, w_down, b_down.reshape(E, 1, -1), rows)
    return _moe_combine(ys, row_of, gates, x, g, b, alpha, emit_bf16)


def _rope_tables(positions):
    inv_freq = ROPE_THETA ** (-jnp.arange(0, ROPE, 2, dtype=F32) / ROPE)
    ang = positions.reshape(-1).astype(F32)[:, None] * inv_freq[None, :]
    cos, sin = jnp.cos(ang), jnp.sin(ang)
    zero = jnp.zeros_like(cos)
    return (jnp.concatenate([cos, cos, zero, zero], axis=-1),
            jnp.concatenate([-sin, sin, zero, zero], axis=-1))


def kernel(x, positions, mla_w_in, mla_g_q, mla_g_kv, mla_w_qb, mla_w_kvb, mla_w_o, moba_w_qkv, moba_w_o, ln1_g, ln1_b, ln2_g, ln2_b, moe_w_router, moe_b_router, moe_w_gu, moe_b_gu, moe_w_down, moe_b_down):
    B, S, D = x.shape
    T = B * S
    depth = ln1_g.shape[0]
    alpha = float((2.0 * depth) ** 0.25)
    H = HEADS

    c_tab, s_tab = _rope_tables(positions)
    pos_f = positions.astype(F32)
    pos_col = pos_f.reshape(B, S, 1)
    pos_blk = pos_f.reshape(B, S // MOBA_BLOCK, MOBA_BLOCK)
    slopes = 2.0 ** (-8.0 * jnp.arange(1, H + 1, dtype=F32) / H)

    xf = x.reshape(T, D)
    xb = xf.astype(BF16)
    for i in range(depth):
        j = i // 2
        if i % 2 == 0:
            n_in = Q_LORA + KV_LORA + ROPE
            w_in_p = jnp.pad(mla_w_in[j], ((0, 0), (0, Q_LORA + KV_LORA + LANES - n_in))).astype(BF16)
            w_q_p = jnp.pad(mla_w_qb[j].reshape(Q_LORA, H, NOPE + ROPE),
                            ((0, 0), (0, 0), (0, HEAD_PAD - NOPE - ROPE))
                            ).reshape(Q_LORA, H * HEAD_PAD).astype(BF16)
            cq_n, ckv_n, kr = _mla_in_proj(xb, w_in_p, mla_g_q[j][None], mla_g_kv[j][None], c_tab, s_tab)
            q = _mla_q_proj(cq_n, w_q_p, c_tab, s_tab, B, S)
            k, v = _mla_kv_proj(ckv_n, mla_w_kvb[j].astype(BF16), kr, B, S)
            o = _mla_attention(q, k, v)
            w_o = mla_w_o[j]
        else:
            qkv, colmean = _moba_qkv_proj(xb, moba_w_qkv[j].astype(BF16), B, S)
            o = _moba_attention(qkv, colmean, pos_col, pos_blk, slopes)
            w_o = moba_w_o[j]
        wr = jnp.pad(moe_w_router[i], ((0, 0), (0, LANES - N_EXPERTS)))
        wr_hi = wr.astype(BF16)
        wr_lo = (wr - wr_hi.astype(F32)).astype(BF16)
        br = jnp.pad(moe_b_router[i], (0, LANES - N_EXPERTS))[None]
        xf, xp, idx, gates = _attn_out(o.reshape(T, H * V_DIM), w_o.astype(BF16), xf,
                                       ln1_g[i][None], ln1_b[i][None], wr_hi, wr_lo, br, alpha)
        xf, xb = _moe_ffn(xp, idx[:, :TOP_K], gates, xf, moe_w_gu[i], moe_b_gu[i],
                          moe_w_down[i], moe_b_down[i], ln2_g[i][None], ln2_b[i][None],
                          alpha, emit_bf16=(i + 1 < depth))
    return xf.reshape(B, S, D)
```

```python
import functools

import jax
import jax.numpy as jnp
from jax import lax
from jax.experimental import pallas as pl
from jax.experimental.pallas import tpu as pltpu

F32 = jnp.float32
BF16 = jnp.bfloat16
I32 = jnp.int32
U32 = jnp.uint32

HEADS = 16
Q_LORA = 512
KV_LORA = 512
NOPE = 128
ROPE = 64
V_DIM = 128
ROPE_THETA = 10000.0
MOBA_DH = 128
MOBA_BLOCK = 256
MOBA_TOP = 3
N_EXPERTS = 32
TOP_K = 4
SWIGLU_LIMIT = 7.0
SWIGLU_ALPHA = 1.702
NEG_INF = -1e30
LN_EPS = 1e-5
RMS_EPS = 1e-6

LANES = 128
HEAD_PAD = 256
VMEM_LIMIT = 52 * 1024 * 1024

AUX_A = 120
AUX_B = 123

TM_PROJ = 256
TQ_MLA = 512
MOBA_TQ = 512
MOE_ROWS = 256
FF_CHUNK = 512
TM_COMBINE = 128


def _cparams(sem):
    return pltpu.CompilerParams(dimension_semantics=sem, vmem_limit_bytes=VMEM_LIMIT)


def _dot(a, b):
    return jnp.dot(a, b, preferred_element_type=F32)


def _dot_nt(a, b):
    return lax.dot_general(a, b, (((1,), (1,)), ((), ())), preferred_element_type=F32)


def _rope_rotate(r, c_tab, s_tab):
    swapped = pltpu.roll(r, 32, axis=1) + pltpu.roll(r, 96, axis=1)
    return r * c_tab + swapped * s_tab


def _layer_norm(z, g, b):
    mu = jnp.mean(z, axis=-1, keepdims=True)
    zc = z - mu
    var = jnp.mean(zc * zc, axis=-1, keepdims=True)
    return zc * lax.rsqrt(var + LN_EPS) * g + b


def _split3(c):
    hi = c.astype(BF16).astype(F32)
    rem = c - hi
    mid = rem.astype(BF16).astype(F32)
    lo = (rem - mid).astype(BF16).astype(F32)
    return hi, mid, lo


def _mla_in_kernel(x_ref, w_ref, gq_ref, gkv_ref, c_ref, s_ref, cq_ref, ckv_ref, kr_ref):
    acc = _dot(x_ref[...], w_ref[...])
    cq = acc[:, :Q_LORA]
    ckv = acc[:, Q_LORA:Q_LORA + KV_LORA]
    kr = acc[:, Q_LORA + KV_LORA:]
    cq_n = cq * lax.rsqrt(jnp.mean(cq * cq, axis=-1, keepdims=True) + RMS_EPS) * gq_ref[...]
    ckv_n = ckv * lax.rsqrt(jnp.mean(ckv * ckv, axis=-1, keepdims=True) + RMS_EPS) * gkv_ref[...]
    cq_ref[...] = cq_n.astype(BF16)
    ckv_ref[...] = ckv_n.astype(BF16)
    kr_ref[...] = _rope_rotate(kr, c_ref[...], s_ref[...]).astype(BF16)


def _mla_in_proj(xb, w_in_p, g_q, g_kv, c_tab, s_tab):
    T, D = xb.shape
    n_out = w_in_p.shape[1]
    tm = TM_PROJ
    return pl.pallas_call(
        _mla_in_kernel,
        grid=(T // tm,),
        in_specs=[
            pl.BlockSpec((tm, D), lambda i: (i, 0)),
            pl.BlockSpec((D, n_out), lambda i: (0, 0)),
            pl.BlockSpec((1, Q_LORA), lambda i: (0, 0)),
            pl.BlockSpec((1, KV_LORA), lambda i: (0, 0)),
            pl.BlockSpec((tm, LANES), lambda i: (i, 0)),
            pl.BlockSpec((tm, LANES), lambda i: (i, 0)),
        ],
        out_specs=[
            pl.BlockSpec((tm, Q_LORA), lambda i: (i, 0)),
            pl.BlockSpec((tm, KV_LORA), lambda i: (i, 0)),
            pl.BlockSpec((tm, LANES), lambda i: (i, 0)),
        ],
        out_shape=[
            jax.ShapeDtypeStruct((T, Q_LORA), BF16),
            jax.ShapeDtypeStruct((T, KV_LORA), BF16),
            jax.ShapeDtypeStruct((T, LANES), BF16),
        ],
        compiler_params=_cparams(("arbitrary",)),
        name="mla_in_proj",
    )(xb, w_in_p, g_q, g_kv, c_tab, s_tab)


def _mla_q_kernel(cq_ref, w_ref, c_ref, s_ref, q_ref, *, scale):
    acc = _dot(cq_ref[...], w_ref[...])
    c_tab = c_ref[...]
    s_tab = s_ref[...]
    for h in range(HEADS):
        base = h * HEAD_PAD
        q_ref[0, h, :, :NOPE] = (acc[:, base:base + NOPE] * scale).astype(BF16)
        rot = _rope_rotate(acc[:, base + NOPE:base + HEAD_PAD], c_tab, s_tab)
        q_ref[0, h, :, NOPE:] = (rot * scale).astype(BF16)


def _mla_q_proj(cq_n, w_q_p, c_tab, s_tab, B, S):
    T = cq_n.shape[0]
    tm = TM_PROJ
    spb = S // tm
    scale = float((NOPE + ROPE) ** -0.5)
    return pl.pallas_call(
        functools.partial(_mla_q_kernel, scale=scale),
        grid=(T // tm,),
        in_specs=[
            pl.BlockSpec((tm, Q_LORA), lambda i: (i, 0)),
            pl.BlockSpec((Q_LORA, HEADS * HEAD_PAD), lambda i: (0, 0)),
            pl.BlockSpec((tm, LANES), lambda i: (i, 0)),
            pl.BlockSpec((tm, LANES), lambda i: (i, 0)),
        ],
        out_specs=pl.BlockSpec((1, HEADS, tm, HEAD_PAD), lambda i: (i // spb, 0, i % spb, 0)),
        out_shape=jax.ShapeDtypeStruct((B, HEADS, S, HEAD_PAD), BF16),
        compiler_params=_cparams(("arbitrary",)),
        name="mla_q_proj",
    )(cq_n, w_q_p, c_tab, s_tab)


def _mla_kv_kernel(ckv_ref, w_ref, kr_ref, k_ref, v_ref):
    acc = _dot(ckv_ref[...], w_ref[...])
    kr = kr_ref[...]
    for h in range(HEADS):
        base = h * (NOPE + V_DIM)
        k_ref[0, h, :, :NOPE] = acc[:, base:base + NOPE].astype(BF16)
        k_ref[0, h, :, NOPE:] = kr
        v_ref[0, h] = acc[:, base + NOPE:base + NOPE + V_DIM].astype(BF16)


def _mla_kv_proj(ckv_n, w_kvb, kr, B, S):
    T = ckv_n.shape[0]
    tm = TM_PROJ
    spb = S // tm
    n_out = HEADS * (NOPE + V_DIM)
    return pl.pallas_call(
        _mla_kv_kernel,
        grid=(T // tm,),
        in_specs=[
            pl.BlockSpec((tm, KV_LORA), lambda i: (i, 0)),
            pl.BlockSpec((KV_LORA, n_out), lambda i: (0, 0)),
            pl.BlockSpec((tm, LANES), lambda i: (i, 0)),
        ],
        out_specs=[
            pl.BlockSpec((1, HEADS, tm, HEAD_PAD), lambda i: (i // spb, 0, i % spb, 0)),
            pl.BlockSpec((1, HEADS, tm, V_DIM), lambda i: (i // spb, 0, i % spb, 0)),
        ],
        out_shape=[
            jax.ShapeDtypeStruct((B, HEADS, S, HEAD_PAD), BF16),
            jax.ShapeDtypeStruct((B, HEADS, S, V_DIM), BF16),
        ],
        compiler_params=_cparams(("arbitrary",)),
        name="mla_kv_proj",
    )(ckv_n, w_kvb, kr)


def _mla_attn_kernel(q_ref, k_ref, v_ref, o_ref, *, tq):
    qi = pl.program_id(2)
    q = q_ref[0, 0]

    def step(kj, carry, diagonal):
        m, l, acc = carry
        start = pl.multiple_of(kj * tq, tq)
        k = k_ref[0, 0, pl.ds(start, tq), :]
        v = v_ref[0, 0, pl.ds(start, tq), :]
        s = _dot_nt(q, k)
        if diagonal:
            row = lax.broadcasted_iota(I32, s.shape, 0)
            col = lax.broadcasted_iota(I32, s.shape, 1)
            s = jnp.where(col <= row, s, NEG_INF)
        m_new = jnp.maximum(m, jnp.max(s, axis=-1, keepdims=True))
        alpha = jnp.exp(m - m_new)
        p = jnp.exp(s - m_new)
        l = alpha * l + jnp.sum(p, axis=-1, keepdims=True)
        acc = alpha * acc + _dot(p.astype(BF16), v)
        return m_new, l, acc

    init = (jnp.full((tq, 1), NEG_INF, F32), jnp.zeros((tq, 1), F32), jnp.zeros((tq, V_DIM), F32))
    carry = lax.fori_loop(0, qi, lambda kj, c: step(kj, c, False), init)
    _, l, acc = step(qi, carry, True)
    o_ref[0] = (acc / l).astype(BF16)


def _mla_attention(q, k, v):
    B, H, S, _ = q.shape
    tq = min(TQ_MLA, S)
    return pl.pallas_call(
        functools.partial(_mla_attn_kernel, tq=tq),
        grid=(B, H, S // tq),
        in_specs=[
            pl.BlockSpec((1, 1, tq, HEAD_PAD), lambda b, h, i: (b, h, i, 0)),
            pl.BlockSpec((1, 1, S, HEAD_PAD), lambda b, h, i: (b, h, 0, 0)),
            pl.BlockSpec((1, 1, S, V_DIM), lambda b, h, i: (b, h, 0, 0)),
        ],
        out_specs=pl.BlockSpec((1, tq, V_DIM), lambda b, h, i: (b, i, h)),
        out_shape=jax.ShapeDtypeStruct((B, S, H * V_DIM), BF16),
        compiler_params=_cparams(("arbitrary", "arbitrary", "arbitrary")),
        name="mla_attention",
    )(q, k, v)


def _moba_qkv_kernel(slope_ref, x_ref, wq_ref, wk_ref, wv_ref, pos_ref, q_ref, k_ref, v_ref, cm_ref,
                     *, tm, hpt, scale, tiles_per_seq):
    hg = pl.program_id(0)
    i = pl.program_id(1)
    x = x_ref[...]
    qa = _dot(x, wq_ref[...])
    ka = _dot(x, wk_ref[...])
    va = _dot(x, wv_ref[...])
    pk = pos_ref[...]
    lane = lax.broadcasted_iota(I32, (tm, LANES), 1)
    row = lax.broadcasted_iota(I32, (tm, LANES), 0)
    blk = (i % tiles_per_seq) * (tm // MOBA_BLOCK) + row // MOBA_BLOCK
    ones_or_onehot = jnp.where((lane >= AUX_A) & (lane < AUX_A + 3), 1.0,
                               jnp.where(lane == blk, 1.0, 0.0))
    for h in range(hpt):
        c_hi, c_mid, c_lo = _split3(slope_ref[hg * hpt + h] * pk)
        aux = jnp.where(lane == AUX_B, c_hi,
                        jnp.where(lane == AUX_B + 1, c_mid,
                                  jnp.where(lane == AUX_B + 2, c_lo, ones_or_onehot)))
        sl = slice(h * MOBA_DH, (h + 1) * MOBA_DH)
        q_ref[0, h] = (qa[:, sl] * scale).astype(BF16)
        k_ref[0, h, :, :MOBA_DH] = ka[:, sl].astype(BF16)
        k_ref[0, h, :, MOBA_DH:] = aux.astype(BF16)
        v_ref[0, h] = va[:, sl].astype(BF16)
    for r in range(tm // MOBA_BLOCK):
        cm_ref[r] = jnp.mean(ka[r * MOBA_BLOCK:(r + 1) * MOBA_BLOCK], axis=0, keepdims=True)


def _moba_qkv_proj(xb, w_qkv_b, pos_col, slopes, B, S):
    T, D = xb.shape
    tm = MOBA_TQ
    hpt = 4
    tn = hpt * MOBA_DH
    ngrp = HEADS // hpt
    spb = S // tm
    nbt = tm // MOBA_BLOCK
    scale = float(MOBA_DH ** -0.5)
    grid_spec = pltpu.PrefetchScalarGridSpec(
        num_scalar_prefetch=1,
        grid=(ngrp, T // tm),
        in_specs=[
            pl.BlockSpec((tm, D), lambda g, i, sl: (i, 0)),
            pl.BlockSpec((D, tn), lambda g, i, sl: (0, g)),
            pl.BlockSpec((D, tn), lambda g, i, sl: (0, ngrp + g)),
            pl.BlockSpec((D, tn), lambda g, i, sl: (0, 2 * ngrp + g)),
            pl.BlockSpec((tm, 1), lambda g, i, sl: (i, 0)),
        ],
        out_specs=[
            pl.BlockSpec((1, hpt, tm, MOBA_DH), lambda g, i, sl: (i // spb, g, i % spb, 0)),
            pl.BlockSpec((1, hpt, tm, 2 * MOBA_DH), lambda g, i, sl: (i // spb, g, i % spb, 0)),
            pl.BlockSpec((1, hpt, tm, MOBA_DH), lambda g, i, sl: (i // spb, g, i % spb, 0)),
            pl.BlockSpec((nbt, 1, tn), lambda g, i, sl: (i, 0, g)),
        ],
    )
    return pl.pallas_call(
        functools.partial(_moba_qkv_kernel, tm=tm, hpt=hpt, scale=scale, tiles_per_seq=spb),
        grid_spec=grid_spec,
        out_shape=[
            jax.ShapeDtypeStruct((B, HEADS, S, MOBA_DH), BF16),
            jax.ShapeDtypeStruct((B, HEADS, S, 2 * MOBA_DH), BF16),
            jax.ShapeDtypeStruct((B, HEADS, S, MOBA_DH), BF16),
            jax.ShapeDtypeStruct((T // MOBA_BLOCK, 1, HEADS * MOBA_DH), F32),
        ],
        compiler_params=_cparams(("arbitrary", "arbitrary")),
        name="moba_qkv_proj",
    )(slopes, xb, w_qkv_b, w_qkv_b, w_qkv_b, pos_col)


def _moba_attn_kernel(slope_ref, q_ref, k_ref, v_ref, km_ref, pq_ref, o_ref, qa_s, km_s, *, nb, tq):
    h = pl.program_id(1)
    qi = pl.program_id(2)
    q = q_ref[0, 0]
    blocks_per_tile = tq // MOBA_BLOCK

    km_s[...] = jnp.zeros(km_s.shape, F32)
    km_s[0:nb] = km_ref[:, 0, :]
    km_hi, km_mid, km_lo = _split3(km_s[...])
    gate = (_dot_nt(q, km_hi.astype(BF16)) + _dot_nt(q, km_mid.astype(BF16))
            + _dot_nt(q, km_lo.astype(BF16)))
    blk = lax.broadcasted_iota(I32, gate.shape, 1)
    row = lax.broadcasted_iota(I32, gate.shape, 0)
    own = qi * blocks_per_tile + row // MOBA_BLOCK
    past = blk < own
    work = jnp.where(past, gate, NEG_INF)
    keep = blk == own
    for _ in range(min(MOBA_TOP, nb)):
        mx = jnp.max(work, axis=-1, keepdims=True)
        ix = jnp.min(jnp.where(work == mx, blk, LANES), axis=-1, keepdims=True)
        hit = blk == ix
        keep = keep | (hit & past)
        work = jnp.where(hit, -jnp.inf, work)

    a_hi, a_mid, a_lo = _split3(-slope_ref[h] * pq_ref[...])
    aux = jnp.where(blk == AUX_A, a_hi,
                    jnp.where(blk == AUX_A + 1, a_mid,
                              jnp.where(blk == AUX_A + 2, a_lo,
                                        jnp.where((blk >= AUX_B) & (blk < AUX_B + 3), 1.0,
                                                  jnp.where(keep, 0.0, NEG_INF)))))
    qa_s[:, :MOBA_DH] = q
    qa_s[:, MOBA_DH:] = aux.astype(BF16)
    qa = qa_s[...]

    def chunk(kj):
        start = pl.multiple_of(kj * tq, tq)
        k = k_ref[0, 0, pl.ds(start, tq), :]
        v = v_ref[0, 0, pl.ds(start, tq), :]
        return _dot_nt(qa, k), v

    s, v = chunk(qi)
    r2 = lax.broadcasted_iota(I32, s.shape, 0)
    c2 = lax.broadcasted_iota(I32, s.shape, 1)
    s = jnp.where(c2 <= r2, s, NEG_INF)
    m0 = jnp.max(s, axis=-1, keepdims=True)
    p = jnp.exp(s - m0)
    l0 = jnp.sum(p, axis=-1, keepdims=True)
    acc0 = _dot(p.astype(BF16), v)

    def past_chunk(kj, carry):
        m, l, acc = carry
        s, v = chunk(kj)
        m_new = jnp.maximum(m, jnp.max(s, axis=-1, keepdims=True))
        alpha = jnp.exp(m - m_new)
        p = jnp.exp(s - m_new)
        l = alpha * l + jnp.sum(p, axis=-1, keepdims=True)
        acc = alpha * acc + _dot(p.astype(BF16), v)
        return m_new, l, acc

    _, l, acc = lax.fori_loop(0, qi, past_chunk, (m0, l0, acc0))
    o_ref[0] = (acc / l).astype(BF16)


def _moba_attention(q, k, v, colmean, pos_col, slopes):
    B, H, S, Dh = q.shape
    tq = MOBA_TQ
    nb = S // MOBA_BLOCK
    spb = S // tq
    grid_spec = pltpu.PrefetchScalarGridSpec(
        num_scalar_prefetch=1,
        grid=(B, H, spb),
        in_specs=[
            pl.BlockSpec((1, 1, tq, Dh), lambda b, h, i, sl: (b, h, i, 0)),
            pl.BlockSpec((1, 1, S, 2 * Dh), lambda b, h, i, sl: (b, h, 0, 0)),
            pl.BlockSpec((1, 1, S, Dh), lambda b, h, i, sl: (b, h, 0, 0)),
            pl.BlockSpec((nb, 1, Dh), lambda b, h, i, sl: (b, 0, h)),
            pl.BlockSpec((tq, 1), lambda b, h, i, sl: (b * spb + i, 0)),
        ],
        out_specs=pl.BlockSpec((1, tq, Dh), lambda b, h, i, sl: (b, i, h)),
        scratch_shapes=[pltpu.VMEM((tq, 2 * Dh), BF16), pltpu.VMEM((LANES, Dh), F32)],
    )
    return pl.pallas_call(
        functools.partial(_moba_attn_kernel, nb=nb, tq=tq),
        grid_spec=grid_spec,
        out_shape=jax.ShapeDtypeStruct((B, S, H * Dh), BF16),
        compiler_params=_cparams(("arbitrary", "arbitrary", "arbitrary")),
        name="moba_attention",
    )(slopes, q, k, v, colmean, pos_col)


def _route(xn, wrh_ref, wrl_ref, br_ref, idx_ref, gate_ref):
    x_hi = xn.astype(BF16)
    x_lo = (xn - x_hi.astype(F32)).astype(BF16)
    wrh = wrh_ref[...]
    logits = _dot(x_hi, wrh) + _dot(x_lo, wrh) + _dot(x_hi, wrl_ref[...]) + br_ref[...]
    lane = lax.broadcasted_iota(I32, logits.shape, 1)
    work = jnp.where(lane < N_EXPERTS, logits, -jnp.inf)
    vals, idxs = [], []
    for _ in range(TOP_K):
        mx = jnp.max(work, axis=-1, keepdims=True)
        ix = jnp.min(jnp.where(work == mx, lane, LANES), axis=-1, keepdims=True)
        vals.append(mx)
        idxs.append(ix)
        work = jnp.where(lane == ix, -jnp.inf, work)
    exps = [jnp.exp(v - vals[0]) for v in vals]
    den = exps[0] + exps[1] + exps[2] + exps[3]
    idx_out = jnp.zeros(logits.shape, I32)
    gate_out = jnp.zeros(logits.shape, F32)
    for k in range(TOP_K):
        idx_out = jnp.where(lane == k, idxs[k], idx_out)
        gate_out = jnp.where(lane == k, exps[k] / den, gate_out)
    idx_ref[...] = idx_out
    gate_ref[...] = gate_out


def _pack_bf16_pairs(xn):
    half = xn.shape[1] // 2
    bits = lax.bitcast_convert_type(xn.astype(BF16).astype(F32), U32)
    return (bits[:, :half] >> 16) | (bits[:, half:] & jnp.uint32(0xFFFF0000))


def _attn_out_kernel(o_ref, wo_ref, x_ref, g_ref, b_ref, wrh_ref, wrl_ref, br_ref,
                     xo_ref, xp_ref, idx_ref, gate_ref, *, alpha):
    y = _dot(o_ref[...], wo_ref[...])
    xn = _layer_norm(alpha * x_ref[...] + y, g_ref[...], b_ref[...])
    xo_ref[...] = xn
    xp_ref[...] = _pack_bf16_pairs(xn)
    _route(xn, wrh_ref, wrl_ref, br_ref, idx_ref, gate_ref)


def _attn_out(o, w_o_b, x, g, b, wr_hi, wr_lo, br, alpha):
    T, D = x.shape
    tm = TM_PROJ
    row = lambda i: (i, 0)
    const = lambda i: (0, 0)
    return pl.pallas_call(
        functools.partial(_attn_out_kernel, alpha=alpha),
        grid=(T // tm,),
        in_specs=[
            pl.BlockSpec((tm, D), row),
            pl.BlockSpec((D, D), const),
            pl.BlockSpec((tm, D), row),
            pl.BlockSpec((1, D), const),
            pl.BlockSpec((1, D), const),
            pl.BlockSpec((D, LANES), const),
            pl.BlockSpec((D, LANES), const),
            pl.BlockSpec((1, LANES), const),
        ],
        out_specs=[
            pl.BlockSpec((tm, D), row),
            pl.BlockSpec((tm, D // 2), row),
            pl.BlockSpec((tm, LANES), row),
            pl.BlockSpec((tm, LANES), row),
        ],
        out_shape=[
            jax.ShapeDtypeStruct((T, D), F32),
            jax.ShapeDtypeStruct((T, D // 2), U32),
            jax.ShapeDtypeStruct((T, LANES), I32),
            jax.ShapeDtypeStruct((T, LANES), F32),
        ],
        compiler_params=_cparams(("arbitrary",)),
        name="attn_out_ln_router",
    )(o, w_o_b, x, g, b, wr_hi, wr_lo, br)


def _moe_gather_kernel(cur_ref, nxt_ref, x_hbm, o_ref, buf, sem, *, rows):
    i = pl.program_id(0)
    n = pl.num_programs(0)
    slot = i % 2

    def issue(idx_ref, dst_slot):
        def body(r, c):
            t = idx_ref[0, 0, r]
            pltpu.make_async_copy(x_hbm.at[pl.ds(t, 1), :], buf.at[dst_slot, pl.ds(r, 1), :],
                                  sem.at[dst_slot]).start()
            return c
        lax.fori_loop(0, rows, body, 0)

    @pl.when(i == 0)
    def _():
        issue(cur_ref, 0)

    @pl.when(i + 1 < n)
    def _():
        issue(nxt_ref, 1 - slot)

    pltpu.make_async_copy(x_hbm.at[pl.ds(0, rows), :], buf.at[slot], sem.at[slot]).wait()
    o_ref[...] = buf[slot]


def _moe_gather(xp, row_tok, rows):
    n_blocks = row_tok.shape[0] // rows
    width = xp.shape[1]
    tok3 = row_tok.reshape(n_blocks, 1, rows)
    return pl.pallas_call(
        functools.partial(_moe_gather_kernel, rows=rows),
        grid=(n_blocks,),
        in_specs=[
            pl.BlockSpec((1, 1, rows), lambda i: (i, 0, 0), memory_space=pltpu.SMEM),
            pl.BlockSpec((1, 1, rows), lambda i: (jnp.minimum(i + 1, n_blocks - 1), 0, 0),
                         memory_space=pltpu.SMEM),
            pl.BlockSpec(memory_space=pl.ANY),
        ],
        out_specs=pl.BlockSpec((rows, width), lambda i: (i, 0)),
        out_shape=jax.ShapeDtypeStruct((n_blocks * rows, width), U32),
        scratch_shapes=[pltpu.VMEM((2, rows, width), U32), pltpu.SemaphoreType.DMA((2,))],
        compiler_params=_cparams(("arbitrary",)),
        name="moe_gather",
    )(tok3, tok3, xp)


def _moe_up_kernel(be_ref, xs_ref, wg_ref, wl_ref, bg_ref, bl_ref, h_ref, wg_s, wl_s):
    i = pl.program_id(1)
    e = be_ref[i]
    prev = be_ref[jnp.maximum(i - 1, 0)]

    @pl.when((i == 0) | (e != prev))
    def _():
        wg_s[...] = wg_ref[0, 0].astype(BF16)
        wl_s[...] = wl_ref[0, 0].astype(BF16)

    u = xs_ref[...]
    half = u.shape[1]
    x_lo = lax.bitcast_convert_type(u << 16, F32).astype(BF16)
    x_hi = lax.bitcast_convert_type(u & jnp.uint32(0xFFFF0000), F32).astype(BF16)
    g = _dot(x_lo, wg_s[:half]) + _dot(x_hi, wg_s[half:]) + bg_ref[0, 0]
    lin = _dot(x_lo, wl_s[:half]) + _dot(x_hi, wl_s[half:]) + bl_ref[0, 0]
    g = jnp.minimum(g, SWIGLU_LIMIT)
    lin = jnp.clip(lin, -SWIGLU_LIMIT, SWIGLU_LIMIT)
    act = g * jax.nn.sigmoid(SWIGLU_ALPHA * g) * (lin + 1.0)
    h_ref[...] = act.astype(BF16)


def _moe_up(xs, block_e, w_gu, b_gu4, layer, rows):
    R, half = xs.shape
    _, E, D, two_ff = w_gu.shape
    ff = two_ff // 2
    fc = FF_CHUNK
    nj = ff // fc
    grid_spec = pltpu.PrefetchScalarGridSpec(
        num_scalar_prefetch=1,
        grid=(nj, R // rows),
        in_specs=[
            pl.BlockSpec((rows, half), lambda j, i, be: (i, 0)),
            pl.BlockSpec((1, 1, D, fc), lambda j, i, be: (layer, be[i], 0, j)),
            pl.BlockSpec((1, 1, D, fc), lambda j, i, be: (layer, be[i], 0, nj + j)),
            pl.BlockSpec((1, 1, 1, fc), lambda j, i, be: (layer, be[i], 0, j)),
            pl.BlockSpec((1, 1, 1, fc), lambda j, i, be: (layer, be[i], 0, nj + j)),
        ],
        out_specs=pl.BlockSpec((rows, fc), lambda j, i, be: (i, j)),
        scratch_shapes=[pltpu.VMEM((D, fc), BF16), pltpu.VMEM((D, fc), BF16)],
    )
    return pl.pallas_call(
        _moe_up_kernel,
        grid_spec=grid_spec,
        out_shape=jax.ShapeDtypeStruct((R, ff), BF16),
        compiler_params=_cparams(("arbitrary", "arbitrary")),
        name="moe_up",
    )(block_e, xs, w_gu, w_gu, b_gu4, b_gu4)


def _moe_down_kernel(be_ref, h_ref, wd_ref, bd_ref, y_ref, wd_s):
    i = pl.program_id(0)
    e = be_ref[i]
    prev = be_ref[jnp.maximum(i - 1, 0)]

    @pl.when((i == 0) | (e != prev))
    def _():
        wd_s[...] = wd_ref[0, 0].astype(BF16)

    y_ref[...] = _dot(h_ref[...], wd_s[...]) + bd_ref[0, 0]


def _moe_down(h, block_e, w_down, b_down4, layer, rows):
    R, ff = h.shape
    D = w_down.shape[-1]
    grid_spec = pltpu.PrefetchScalarGridSpec(
        num_scalar_prefetch=1,
        grid=(R // rows,),
        in_specs=[
            pl.BlockSpec((rows, ff), lambda i, be: (i, 0)),
            pl.BlockSpec((1, 1, ff, D), lambda i, be: (layer, be[i], 0, 0)),
            pl.BlockSpec((1, 1, 1, D), lambda i, be: (layer, be[i], 0, 0)),
        ],
        out_specs=pl.BlockSpec((rows, D), lambda i, be: (i, 0)),
        scratch_shapes=[pltpu.VMEM((ff, D), BF16)],
    )
    return pl.pallas_call(
        _moe_down_kernel,
        grid_spec=grid_spec,
        out_shape=jax.ShapeDtypeStruct((R, D), F32),
        compiler_params=_cparams(("arbitrary",)),
        name="moe_down",
    )(block_e, h, w_down, b_down4)


def _moe_combine_kernel(cur_ref, nxt_ref, ys_hbm, gate_ref, x_ref, g_ref, b_ref, *rest,
                        tm, alpha, emit_bf16):
    if emit_bf16:
        xo_ref, xb_ref, buf, sem = rest
    else:
        xo_ref, buf, sem = rest
        xb_ref = None
    i = pl.program_id(0)
    n = pl.num_programs(0)
    slot = i % 2

    def issue(idx_ref, dst_slot):
        def body(r, c):
            for k in range(TOP_K):
                row = idx_ref[0, 0, r * TOP_K + k]
                pltpu.make_async_copy(ys_hbm.at[pl.ds(row, 1), :],
                                      buf.at[dst_slot, k, pl.ds(r, 1), :],
                                      sem.at[dst_slot]).start()
            return c
        lax.fori_loop(0, tm, body, 0)

    @pl.when(i == 0)
    def _():
        issue(cur_ref, 0)

    @pl.when(i + 1 < n)
    def _():
        issue(nxt_ref, 1 - slot)

    for k in range(TOP_K):
        pltpu.make_async_copy(ys_hbm.at[pl.ds(0, tm), :], buf.at[slot, k], sem.at[slot]).wait()

    gates = gate_ref[...]
    y = gates[:, 0:1] * buf[slot, 0]
    for k in range(1, TOP_K):
        y = y + gates[:, k:k + 1] * buf[slot, k]
    xn = _layer_norm(alpha * x_ref[...] + y, g_ref[...], b_ref[...])
    xo_ref[...] = xn
    if emit_bf16:
        xb_ref[...] = xn.astype(BF16)


def _moe_combine(ys, row_of, gates, x, g, b, alpha, emit_bf16):
    T, D = x.shape
    tm = TM_COMBINE
    n = T // tm
    idx3 = row_of.reshape(n, 1, tm * TOP_K)
    row = lambda i: (i, 0)
    const = lambda i: (0, 0)
    out_specs = [pl.BlockSpec((tm, D), row)]
    out_shape = [jax.ShapeDtypeStruct((T, D), F32)]
    if emit_bf16:
        out_specs.append(pl.BlockSpec((tm, D), row))
        out_shape.append(jax.ShapeDtypeStruct((T, D), BF16))
    outs = pl.pallas_call(
        functools.partial(_moe_combine_kernel, tm=tm, alpha=alpha, emit_bf16=emit_bf16),
        grid=(n,),
        in_specs=[
            pl.BlockSpec((1, 1, tm * TOP_K), lambda i: (i, 0, 0), memory_space=pltpu.SMEM),
            pl.BlockSpec((1, 1, tm * TOP_K), lambda i: (jnp.minimum(i + 1, n - 1), 0, 0),
                         memory_space=pltpu.SMEM),
            pl.BlockSpec(memory_space=pl.ANY),
            pl.BlockSpec((tm, LANES), row),
            pl.BlockSpec((tm, D), row),
            pl.BlockSpec((1, D), const),
            pl.BlockSpec((1, D), const),
        ],
        out_specs=out_specs,
        out_shape=out_shape,
        scratch_shapes=[pltpu.VMEM((2, TOP_K, tm, D), F32), pltpu.SemaphoreType.DMA((2,))],
        compiler_params=_cparams(("arbitrary",)),
        name="moe_combine_ln",
    )(idx3, idx3, ys, gates, x, g, b)
    return outs if emit_bf16 else (outs[0], None)


def _group_rows(top_idx, rows):
    T = top_idx.shape[0]
    A = T * TOP_K
    flat = top_idx.reshape(A)
    onehot = (flat[:, None] == jnp.arange(N_EXPERTS, dtype=I32)[None, :]).astype(I32)
    csum = jnp.cumsum(onehot, axis=0)
    counts = csum[-1]
    rank = jnp.take_along_axis(csum, flat[:, None], axis=1)[:, 0] - 1
    padded = (counts + rows - 1) // rows * rows
    pend = jnp.cumsum(padded)
    pstart = pend - padded
    row_of = (pstart[flat] + rank).astype(I32)
    n_blocks = A // rows + N_EXPERTS
    row_tok = jnp.zeros((n_blocks * rows,), I32).at[row_of].set(jnp.arange(A, dtype=I32) // TOP_K)
    starts = jnp.arange(n_blocks, dtype=I32) * rows
    block_e = jnp.minimum(jnp.searchsorted(pend, starts, side="right"), N_EXPERTS - 1).astype(I32)
    return row_tok, block_e, row_of


def _moe_ffn(xp, top_idx, gates, x, w_gu, b_gu4, w_down, b_down4, layer, g, b, alpha, emit_bf16):
    rows = MOE_ROWS
    row_tok, block_e, row_of = _group_rows(top_idx, rows)
    xs = _moe_gather(xp, row_tok, rows)
    h = _moe_up(xs, block_e, w_gu, b_gu4, layer, rows)
    ys = _moe_down(h, block_e, w_down, b_down4, layer, rows)
    return _moe_combine(ys, row_of, gates, x, g, b, alpha, emit_bf16)


def _rope_tables(positions):
    inv_freq = ROPE_THETA ** (-jnp.arange(0, ROPE, 2, dtype=F32) / ROPE)
    ang = positions.reshape(-1).astype(F32)[:, None] * inv_freq[None, :]
    cos, sin = jnp.cos(ang), jnp.sin(ang)
    zero = jnp.zeros_like(cos)
    return (jnp.concatenate([cos, cos, zero, zero], axis=-1),
            jnp.concatenate([-sin, sin, zero, zero], axis=-1))


def kernel(x, positions, mla_w_in, mla_g_q, mla_g_kv, mla_w_qb, mla_w_kvb, mla_w_o, moba_w_qkv, moba_w_o, ln1_g, ln1_b, ln2_g, ln2_b, moe_w_router, moe_b_router, moe_w_gu, moe_b_gu, moe_w_down, moe_b_down):
    B, S, D = x.shape
    T = B * S
    depth = ln1_g.shape[0]
    alpha = float((2.0 * depth) ** 0.25)
    H = HEADS
    assert S % MOBA_TQ == 0 and S % TQ_MLA == 0 and S // MOBA_BLOCK <= AUX_A

    c_tab, s_tab = _rope_tables(positions)
    pos_col = positions.astype(F32).reshape(T, 1)
    slopes = 2.0 ** (-8.0 * jnp.arange(1, H + 1, dtype=F32) / H)
    b_gu4 = moe_b_gu.reshape(depth, N_EXPERTS, 1, -1)
    b_down4 = moe_b_down.reshape(depth, N_EXPERTS, 1, -1)

    xf = x.reshape(T, D)
    xb = xf.astype(BF16)
    for i in range(depth):
        j = i // 2
        if i % 2 == 0:
            n_in = Q_LORA + KV_LORA + ROPE
            w_in_p = jnp.pad(mla_w_in[j], ((0, 0), (0, Q_LORA + KV_LORA + LANES - n_in))).astype(BF16)
            w_q_p = jnp.pad(mla_w_qb[j].reshape(Q_LORA, H, NOPE + ROPE),
                            ((0, 0), (0, 0), (0, HEAD_PAD - NOPE - ROPE))
                            ).reshape(Q_LORA, H * HEAD_PAD).astype(BF16)
            cq_n, ckv_n, kr = _mla_in_proj(xb, w_in_p, mla_g_q[j][None], mla_g_kv[j][None], c_tab, s_tab)
            q = _mla_q_proj(cq_n, w_q_p, c_tab, s_tab, B, S)
            k, v = _mla_kv_proj(ckv_n, mla_w_kvb[j].astype(BF16), kr, B, S)
            o = _mla_attention(q, k, v)
            w_o = mla_w_o[j]
        else:
            q, k, v, colmean = _moba_qkv_proj(xb, moba_w_qkv[j].astype(BF16), pos_col, slopes, B, S)
            o = _moba_attention(q, k, v, colmean, pos_col, slopes)
            w_o = moba_w_o[j]
        wr = jnp.pad(moe_w_router[i], ((0, 0), (0, LANES - N_EXPERTS)))
        wr_hi = wr.astype(BF16)
        wr_lo = (wr - wr_hi.astype(F32)).astype(BF16)
        br = jnp.pad(moe_b_router[i], (0, LANES - N_EXPERTS))[None]
        xf, xp, idx, gates = _attn_out(o.reshape(T, H * V_DIM), w_o.astype(BF16), xf,
                                       ln1_g[i][None], ln1_b[i][None], wr_hi, wr_lo, br, alpha)
        xf, xb = _moe_ffn(xp, idx[:, :TOP_K], gates, xf, moe_w_gu, b_gu4, moe_w_down, b_down4, i,
                          ln2_g[i][None], ln2_b[i][None], alpha, emit_bf16=(i + 1 < depth))
    return xf.reshape(B, S, D)
```
